```python
import jax, jax.numpy as jnp
from jax import lax
import numpy as np

D_MODEL = 1024
BATCH = 8
SEQ = 4096
DEPTH = 1

HG_HEADS = 4
HG_KEY_DIM = 128
HG_VAL_DIM = 128
HG_KEY_WIDTH = HG_HEADS * HG_KEY_DIM
HG_WIDTH = HG_HEADS * HG_VAL_DIM
HG_CHUNK = 64
MLA_HEADS = 4
MLA_NOPE_DIM = 128
MLA_ROPE_DIM = 64
MLA_V_DIM = 128
MLA_Q_RANK = 256
MLA_KV_RANK = 256
MLA_WIDTH = MLA_HEADS * MLA_V_DIM
ROPE_THETA = 10000.0
Q_BLOCK = 128
MIX_WIDTH = HG_WIDTH + MLA_WIDTH
D_FF = 4 * D_MODEL
N_MOD = 6
ADA_INIT = 0.5
RMS_EPS = 1e-6
LN_EPS = 1e-5
DN_ALPHA = (2.0 * DEPTH) ** 0.25
DN_BETA = (8.0 * DEPTH) ** -0.25
IN_SIZES = (HG_KEY_WIDTH, HG_KEY_WIDTH, HG_WIDTH, HG_WIDTH, MLA_Q_RANK, MLA_KV_RANK, MLA_ROPE_DIM)
IN_COLS = sum(IN_SIZES)
IN_SPLITS = tuple(int(s) for s in np.cumsum(IN_SIZES)[:-1])

kernel_name = 'hybrid_hgrn2_mla_deepnorm_adaln'


def rms_norm(x, w):
    xf = x.astype(jnp.float32)
    y = xf * lax.rsqrt(jnp.mean(xf * xf, axis=-1, keepdims=True) + RMS_EPS) * w.astype(jnp.float32)
    return y.astype(x.dtype)


def layer_norm(x, g, b):
    xf = x.astype(jnp.float32)
    mu = jnp.mean(xf, axis=-1, keepdims=True)
    xc = xf - mu
    var = jnp.mean(xc * xc, axis=-1, keepdims=True)
    y = xc * lax.rsqrt(var + LN_EPS) * g.astype(jnp.float32) + b.astype(jnp.float32)
    return y.astype(x.dtype)


def rope_tables(positions):
    inv_freq = 1.0 / (ROPE_THETA ** (jnp.arange(0, MLA_ROPE_DIM, 2, dtype=jnp.float32) / MLA_ROPE_DIM))
    ang = positions.astype(jnp.float32)[..., None] * inv_freq
    return jnp.cos(ang), jnp.sin(ang)


def apply_rope(x, cos, sin):
    xf = x.astype(jnp.float32)
    x1, x2 = jnp.split(xf, 2, axis=-1)
    return jnp.concatenate([x1 * cos - x2 * sin, x2 * cos + x1 * sin], axis=-1).astype(x.dtype)


def hgrn2_chunkwise(q, f_logit, v, lb):
    B, T, H, dk = q.shape
    dv = v.shape[-1]
    C = HG_CHUNK
    n = T // C
    f32 = jnp.float32
    forget = lb.astype(f32) + (1.0 - lb.astype(f32)) * jax.nn.sigmoid(f_logit.astype(f32))
    k = 1.0 - forget
    log_f = jnp.log(forget)

    def chunked(a):
        return a.astype(f32).reshape(B, n, C, H, a.shape[-1]).transpose(0, 3, 1, 2, 4)

    q, k, v, log_f = chunked(q), chunked(k), chunked(v), chunked(log_f)
    b = jnp.cumsum(log_f, axis=3)
    b_ref = b[:, :, :, C // 2 - 1:C // 2, :]
    b_last = b[:, :, :, C - 1:C, :]
    causal = jnp.tril(jnp.ones((C, C), dtype=bool))
    a = jnp.einsum('bhncd,bhnsd->bhncs', q * jnp.exp(b - b_ref), k * jnp.exp(b_ref - b))
    a = jnp.where(causal, a, 0.0)
    o_intra = jnp.einsum('bhncs,bhnse->bhnce', a, v)
    kv = jnp.einsum('bhnsd,bhnse->nbhde', k * jnp.exp(b_last - b), v)
    decay = jnp.exp(b_last[:, :, :, 0, :]).transpose(2, 0, 1, 3)

    def step(state, inp):
        d, kv_n = inp
        return d[..., None] * state + kv_n, state

    s0 = jnp.zeros((B, H, dk, dv), f32)
    _, s_prev = lax.scan(step, s0, (decay, kv))
    o_inter = jnp.einsum('bhncd,nbhde->bhnce', q * jnp.exp(b), s_prev)
    return (o_intra + o_inter).transpose(0, 2, 3, 1, 4).reshape(B, T, H, dv)


def mla_causal_attention(q_nope, q_pe, k_nope, k_pe, v):
    B, T, H, _ = q_nope.shape
    nb = T // Q_BLOCK
    scale = (MLA_NOPE_DIM + MLA_ROPE_DIM) ** -0.5
    qn = q_nope.reshape(B, nb, Q_BLOCK, H, MLA_NOPE_DIM).transpose(1, 0, 2, 3, 4)
    qp = q_pe.reshape(B, nb, Q_BLOCK, H, MLA_ROPE_DIM).transpose(1, 0, 2, 3, 4)
    starts = jnp.arange(nb, dtype=jnp.int32) * Q_BLOCK
    key_idx = jnp.arange(T, dtype=jnp.int32)
    neg = jnp.finfo(jnp.float32).min

    def block(args):
        qn_b, qp_b, start = args
        s = (jnp.einsum('bqhd,bkhd->bhqk', qn_b, k_nope).astype(jnp.float32)
             + jnp.einsum('bqhd,bkd->bhqk', qp_b, k_pe).astype(jnp.float32)) * scale
        q_idx = start + jnp.arange(Q_BLOCK, dtype=jnp.int32)
        s = jnp.where(key_idx[None, :] <= q_idx[:, None], s, neg)
        p = jax.nn.softmax(s, axis=-1)
        return jnp.einsum('bhqk,bkhd->bqhd', p.astype(v.dtype), v)

    out = lax.map(block, (qn, qp, starts))
    return out.transpose(1, 0, 2, 3, 4).reshape(B, T, H, MLA_V_DIM)


def hybrid_mixer(u, cos, sin, lb, w_in, hg_norm_w, q_norm_w, w_q_up, kv_norm_w, w_kv_up, w_out):
    B, T, _ = u.shape
    z = u @ w_in
    hq, hf, hi, hg, c_q, c_kv, k_pe = jnp.split(z, IN_SPLITS, axis=-1)
    o_hg = hgrn2_chunkwise(hq.reshape(B, T, HG_HEADS, HG_KEY_DIM),
                           hf.reshape(B, T, HG_HEADS, HG_KEY_DIM),
                           hi.reshape(B, T, HG_HEADS, HG_VAL_DIM),
                           lb.reshape(HG_HEADS, HG_KEY_DIM))
    o_hg = rms_norm(o_hg, hg_norm_w.reshape(HG_HEADS, HG_VAL_DIM))
    o_hg = (o_hg * jax.nn.silu(hg.astype(jnp.float32)).reshape(B, T, HG_HEADS, HG_VAL_DIM))
    o_hg = o_hg.astype(u.dtype).reshape(B, T, HG_WIDTH)
    q = (rms_norm(c_q, q_norm_w) @ w_q_up).reshape(B, T, MLA_HEADS, MLA_NOPE_DIM + MLA_ROPE_DIM)
    q_nope, q_pe = q[..., :MLA_NOPE_DIM], q[..., MLA_NOPE_DIM:]
    kvu = (rms_norm(c_kv, kv_norm_w) @ w_kv_up).reshape(B, T, MLA_HEADS, MLA_NOPE_DIM + MLA_V_DIM)
    k_nope, v = kvu[..., :MLA_NOPE_DIM], kvu[..., MLA_NOPE_DIM:]
    q_pe = apply_rope(q_pe, cos[:, :, None, :], sin[:, :, None, :])
    k_pe = apply_rope(k_pe, cos, sin)
    o_mla = mla_causal_attention(q_nope, q_pe, k_nope, k_pe, v).reshape(B, T, MLA_WIDTH)
    return jnp.concatenate([o_hg, o_mla.astype(u.dtype)], axis=-1) @ w_out


def setup_inputs(seed: int = 0) -> dict:
    key = jax.random.key(seed)
    ks = jax.random.split(key, 20)

    def nrm(k, shape, scale):
        return jax.random.normal(k, shape, jnp.float32) * scale

    x = nrm(ks[0], (BATCH, SEQ, D_MODEL), 1.0)
    c = nrm(ks[1], (BATCH, D_MODEL), 1.0)
    offsets = jax.random.randint(ks[2], (BATCH, 1), 0, 1024, dtype=jnp.int32)
    positions = (jnp.arange(SEQ, dtype=jnp.int32)[None, :] + offsets).astype(jnp.int32)
    return {
        'x': x,
        'c': c,
        'positions': positions,
        'w_ada': nrm(ks[3], (DEPTH, D_MODEL, N_MOD * D_MODEL), ADA_INIT * D_MODEL ** -0.5),
        'b_ada': nrm(ks[4], (DEPTH, N_MOD * D_MODEL), 0.02),
        'w_in': nrm(ks[5], (DEPTH, D_MODEL, IN_COLS), D_MODEL ** -0.5),
        'hg_lower_bounds': nrm(ks[6], (DEPTH + 1, HG_KEY_WIDTH), 0.1),
        'hg_norm_w': 1.0 + nrm(ks[7], (DEPTH, HG_WIDTH), 0.02),
        'mla_q_norm_w': 1.0 + nrm(ks[8], (DEPTH, MLA_Q_RANK), 0.02),
        'w_q_up': nrm(ks[9], (DEPTH, MLA_Q_RANK, MLA_HEADS * (MLA_NOPE_DIM + MLA_ROPE_DIM)), MLA_Q_RANK ** -0.5),
        'mla_kv_norm_w': 1.0 + nrm(ks[10], (DEPTH, MLA_KV_RANK), 0.02),
        'w_kv_up': nrm(ks[11], (DEPTH, MLA_KV_RANK, MLA_HEADS * (MLA_NOPE_DIM + MLA_V_DIM)), MLA_KV_RANK ** -0.5),
        'w_out': nrm(ks[12], (DEPTH, MIX_WIDTH, D_MODEL), DN_BETA * MIX_WIDTH ** -0.5),
        'ln1_g': 1.0 + nrm(ks[13], (DEPTH, D_MODEL), 0.02),
        'ln1_b': nrm(ks[14], (DEPTH, D_MODEL), 0.02),
        'w_mlp_in': nrm(ks[15], (DEPTH, D_MODEL, D_FF), D_MODEL ** -0.5),
        'w_mlp_out': nrm(ks[16], (DEPTH, D_FF, D_MODEL), DN_BETA * D_FF ** -0.5),
        'ln2_g': 1.0 + nrm(ks[17], (DEPTH, D_MODEL), 0.02),
        'ln2_b': nrm(ks[18], (DEPTH, D_MODEL), 0.02),
    }


def reference(x, c, positions, w_ada, b_ada, w_in, hg_lower_bounds, hg_norm_w, mla_q_norm_w, w_q_up,
              mla_kv_norm_w, w_kv_up, w_out, ln1_g, ln1_b, w_mlp_in, w_mlp_out, ln2_g, ln2_b):
    cos, sin = rope_tables(positions)
    lbs = jnp.cumsum(jax.nn.softmax(hg_lower_bounds.astype(jnp.float32), axis=0), axis=0)[:DEPTH]
    cond = jax.nn.silu(c)
    for l in range(DEPTH):
        mod = (cond @ w_ada[l] + b_ada[l])[:, None, :]
        sh_a, sc_a, g_a, sh_m, sc_m, g_m = jnp.split(mod, N_MOD, axis=-1)
        u = x * (1.0 + sc_a) + sh_a
        mix = hybrid_mixer(u, cos, sin, lbs[l], w_in[l], hg_norm_w[l], mla_q_norm_w[l], w_q_up[l],
                           mla_kv_norm_w[l], w_kv_up[l], w_out[l])
        x = layer_norm(DN_ALPHA * x + (1.0 + g_a) * mix, ln1_g[l], ln1_b[l])
        u = x * (1.0 + sc_m) + sh_m
        h = jnp.square(jax.nn.relu(u @ w_mlp_in[l])) @ w_mlp_out[l]
        x = layer_norm(DN_ALPHA * x + (1.0 + g_m) * h, ln2_g[l], ln2_b[l])
    return x
```

```python
import functools
import math

import jax
import jax.numpy as jnp
from jax import lax
from jax.experimental import pallas as pl
from jax.experimental.pallas import tpu as pltpu

F32 = jnp.float32
BF16 = jnp.bfloat16

HG_HEADS = 4
HG_DIM = 128
HG_WIDTH = HG_HEADS * HG_DIM
HG_CHUNK = 64
MLA_HEADS = 4
MLA_NOPE = 128
MLA_ROPE = 64
MLA_V = 128
MLA_RANK = 256
ROPE_THETA = 10000.0
N_MOD = 6
RMS_EPS = 1e-6
LN_EPS = 1e-5

LANES = 128
V7X_VMEM_BYTES = 64 * 1024 * 1024
VMEM_LIMIT = V7X_VMEM_BYTES - 6 * 1024 * 1024

HEAD_SLOT = 2 * LANES
NEG_BIG = -1e30
LOG2E = 1.4426950408889634


def _sigmoid(x):
    return 1.0 / (1.0 + jnp.exp(-x))


def _split_bf16(x):
    hi = x.astype(BF16)
    lo = (x - hi.astype(F32)).astype(BF16)
    return hi, lo


def _dot(a, b):
    return jnp.dot(a, b, preferred_element_type=F32)


def _dot_nt(a, b):
    return lax.dot_general(a, b, (((1,), (1,)), ((), ())), preferred_element_type=F32)


def _dot_tn(a, b):
    return lax.dot_general(a, b, (((0,), (0,)), ((), ())), preferred_element_type=F32)


def _cparams(semantics):
    return pltpu.CompilerParams(dimension_semantics=semantics, vmem_limit_bytes=VMEM_LIMIT)


def _rope_table_kernel(pos_ref, invf_ref, out_ref):
    ang = invf_ref[...] * pos_ref[...].astype(F32)
    c = jnp.cos(ang)
    s = jnp.sin(ang)
    tab = jnp.concatenate([c, c, -s, s], axis=0)
    out_ref[...] = tab.T


def _rope_table(positions):
    B, T = positions.shape
    half = MLA_ROPE // 2
    inv_freq = 1.0 / (ROPE_THETA ** (jnp.arange(0, MLA_ROPE, 2, dtype=F32) / MLA_ROPE))
    return pl.pallas_call(
        _rope_table_kernel,
        grid=(B,),
        in_specs=[pl.BlockSpec((None, 1, T), lambda b: (b, 0, 0)),
                  pl.BlockSpec((half, 1), lambda b: (0, 0))],
        out_specs=pl.BlockSpec((None, T, LANES), lambda b: (b, 0, 0)),
        out_shape=jax.ShapeDtypeStruct((B, T, LANES), F32),
        compiler_params=_cparams(("arbitrary",)),
        name="rope_table",
    )(positions.reshape(B, 1, T), inv_freq.reshape(half, 1))


def _ada_kernel(c_ref, w_ref, b_ref, out_ref):
    c = c_ref[...]
    cond = c * _sigmoid(c)
    ch, cl = _split_bf16(cond)
    wh, wl = _split_bf16(w_ref[...])
    out_ref[...] = _dot(ch, wh) + (_dot(cl, wh) + _dot(ch, wl)) + b_ref[...]


def _ada_mod(c, w_ada, b_ada):
    B, D = c.shape
    n_out = w_ada.shape[1]
    tn = D
    return pl.pallas_call(
        _ada_kernel,
        grid=(n_out // tn,),
        in_specs=[pl.BlockSpec((B, D), lambda j: (0, 0)),
                  pl.BlockSpec((D, tn), lambda j: (0, j)),
                  pl.BlockSpec((1, tn), lambda j: (0, j))],
        out_specs=pl.BlockSpec((B, tn), lambda j: (0, j)),
        out_shape=jax.ShapeDtypeStruct((B, n_out), F32),
        compiler_params=_cparams(("arbitrary",)),
        name="ada_mod",
    )(c, w_ada, b_ada.reshape(1, n_out))


def _rms(v, w):
    return v * lax.rsqrt(jnp.mean(v * v, axis=-1, keepdims=True) + RMS_EPS) * w


def _inproj_kernel(x_ref, mod_ref, win_ref, tab_ref, qnw_ref, kvnw_ref, wq_ref, wkv_ref,
                   hq_ref, hf_ref, hi_ref, hg_ref, q_ref, k_ref, vt_ref, *, q_scale):
    x = x_ref[...]
    sh = mod_ref[0:1, :]
    sc = mod_ref[1:2, :]
    u = (x * (1.0 + sc) + sh).astype(BF16)
    for idx, ref in enumerate((hq_ref, hf_ref, hi_ref, hg_ref)):
        z = _dot(u, win_ref[:, idx * HG_WIDTH:(idx + 1) * HG_WIDTH])
        for h in range(HG_HEADS):
            ref[h] = z[:, h * HG_DIM:(h + 1) * HG_DIM].astype(BF16)
    lat0 = 4 * HG_WIDTH
    zl = _dot(u, win_ref[:, lat0:lat0 + 2 * MLA_RANK + LANES])
    cq = zl[:, :MLA_RANK]
    ckv = zl[:, MLA_RANK:2 * MLA_RANK]
    zk = zl[:, 2 * MLA_RANK:]
    tab = tab_ref[...]

    cqn = _rms(cq, qnw_ref[...]).astype(BF16)
    qf = _dot(cqn, wq_ref[...])
    for h in range(MLA_HEADS):
        base = h * HEAD_SLOT
        q_ref[:, base:base + LANES] = (qf[:, base:base + LANES] * q_scale).astype(BF16)
        t = qf[:, base + LANES:base + HEAD_SLOT] * tab
        r = (t + pltpu.roll(t, MLA_ROPE, 1)) * q_scale
        q_ref[:, base + LANES:base + HEAD_SLOT] = r.astype(BF16)

    ckvn = _rms(ckv, kvnw_ref[...]).astype(BF16)
    kv = _dot(ckvn, wkv_ref[...])
    tk = zk * tab
    lane = lax.broadcasted_iota(jnp.int32, tk.shape, 1)
    kpe = jnp.where(lane < MLA_ROPE, tk + pltpu.roll(tk, MLA_ROPE, 1), 0.0).astype(BF16)
    for h in range(MLA_HEADS):
        base = h * HEAD_SLOT
        k_ref[:, base:base + LANES] = kv[:, h * MLA_NOPE:(h + 1) * MLA_NOPE].astype(BF16)
        k_ref[:, base + LANES:base + HEAD_SLOT] = kpe
    vt_ref[...] = kv[:, MLA_HEADS * MLA_NOPE:].T.astype(BF16)


def _inproj(x, mod3, win, tab, qnw, kvnw, wq, wkv, *, tm):
    B, T, D = x.shape
    q_scale = float((MLA_NOPE + MLA_ROPE) ** -0.5 * LOG2E)
    hshape = jax.ShapeDtypeStruct((B, HG_HEADS, T, HG_DIM), BF16)
    hspec = pl.BlockSpec((None, HG_HEADS, tm, HG_DIM), lambda b, i: (b, 0, i, 0))
    qk_w = MLA_HEADS * HEAD_SLOT
    full = lambda a: pl.BlockSpec(a.shape, lambda b, i: (0,) * a.ndim)
    return pl.pallas_call(
        functools.partial(_inproj_kernel, q_scale=q_scale),
        grid=(B, T // tm),
        in_specs=[pl.BlockSpec((None, tm, D), lambda b, i: (b, i, 0)),
                  pl.BlockSpec((None, N_MOD, D), lambda b, i: (b, 0, 0)),
                  full(win),
                  pl.BlockSpec((None, tm, LANES), lambda b, i: (b, i, 0)),
                  full(qnw), full(kvnw), full(wq), full(wkv)],
        out_specs=[hspec, hspec, hspec, hspec,
                   pl.BlockSpec((None, tm, qk_w), lambda b, i: (b, i, 0)),
                   pl.BlockSpec((None, tm, qk_w), lambda b, i: (b, i, 0)),
                   pl.BlockSpec((None, MLA_HEADS * MLA_V, tm), lambda b, i: (b, 0, i))],
        out_shape=[hshape, hshape, hshape, hshape,
                   jax.ShapeDtypeStruct((B, T, qk_w), BF16),
                   jax.ShapeDtypeStruct((B, T, qk_w), BF16),
                   jax.ShapeDtypeStruct((B, MLA_HEADS * MLA_V, T), BF16)],
        compiler_params=_cparams(("arbitrary", "arbitrary")),
        name="inproj",
    )(x, mod3, win, tab, qnw, kvnw, wq, wkv)


def _hgrn_kernel(hq_ref, hf_ref, hi_ref, hg_ref, lbp_ref, nw_ref, tri_ref, o_ref, st_ref, *, tt):
    @pl.when(pl.program_id(1) == 0)
    def _():
        st_ref[...] = jnp.zeros_like(st_ref)

    lbp = lbp_ref[...]
    e = jnp.exp(lbp - jnp.max(lbp, axis=0, keepdims=True))
    lb_all = e[0:1, :] / jnp.sum(e, axis=0, keepdims=True)
    nw_all = nw_ref[...]
    tri = tri_ref[...]
    C = HG_CHUNK
    row = lax.broadcasted_iota(jnp.int32, (C, C), 0)
    col = lax.broadcasted_iota(jnp.int32, (C, C), 1)
    causal = col <= row

    for h in range(HG_HEADS):
        lb = lb_all[:, h * HG_DIM:(h + 1) * HG_DIM]
        nw = nw_all[:, h * HG_DIM:(h + 1) * HG_DIM]
        q = hq_ref[h].astype(F32)
        fl = hf_ref[h].astype(F32)
        v = hi_ref[h]
        g = hg_ref[h].astype(F32)
        f = lb + (1.0 - lb) * _sigmoid(fl)
        k = 1.0 - f
        lf_hi, lf_lo = _split_bf16(jnp.log(f))
        b = _dot(tri, lf_hi) + _dot(tri, lf_lo)
        st = st_ref[h]
        for c in range(tt // C):
            sl = slice(c * C, (c + 1) * C)
            bc = b[sl]
            bref = bc[C // 2 - 1:C // 2]
            blast = bc[C - 1:C]
            qc = q[sl]
            kc = k[sl]
            vc = v[sl]
            qd = (qc * jnp.exp(bc - bref)).astype(BF16)
            kd = (kc * jnp.exp(bref - bc)).astype(BF16)
            a = jnp.where(causal, _dot_nt(qd, kd), 0.0).astype(BF16)
            o = _dot(a, vc)
            qb = (qc * jnp.exp(bc)).astype(BF16)
            o = o + _dot_nt(qb, st.astype(BF16))
            kl = (kc * jnp.exp(blast - bc)).astype(BF16)
            st = st * jnp.exp(blast) + _dot_tn(vc, kl)
            o = o * lax.rsqrt(jnp.mean(o * o, axis=-1, keepdims=True) + RMS_EPS) * nw
            gc = g[sl]
            o = o * (gc * _sigmoid(gc))
            o_ref[sl, h * HG_DIM:(h + 1) * HG_DIM] = o.astype(o_ref.dtype)
        st_ref[h] = st


def _hgrn(hq, hf, hi, hg, lbp, nw, *, tt):
    B, H, T, d = hq.shape
    C = HG_CHUNK
    r = jnp.arange(tt)
    tri = ((r[:, None] // C == r[None, :] // C) & (r[None, :] <= r[:, None])).astype(BF16)
    hspec = pl.BlockSpec((None, H, tt, d), lambda b, i: (b, 0, i, 0))
    full = lambda a: pl.BlockSpec(a.shape, lambda b, i: (0,) * a.ndim)
    return pl.pallas_call(
        functools.partial(_hgrn_kernel, tt=tt),
        grid=(B, T // tt),
        in_specs=[hspec, hspec, hspec, hspec, full(lbp), full(nw), full(tri)],
        out_specs=pl.BlockSpec((None, tt, H * d), lambda b, i: (b, i, 0)),
        out_shape=jax.ShapeDtypeStruct((B, T, H * d), BF16),
        scratch_shapes=[pltpu.VMEM((H, d, d), F32)],
        compiler_params=_cparams(("arbitrary", "arbitrary")),
        name="hgrn2",
    )(hq, hf, hi, hg, lbp, nw, tri)


def _attn_kernel(q_ref, k_ref, vt_ref, o_ref, qt_scr, m_scr, l_scr, acc_scr, *, tq, tk):
    T = q_ref.shape[0]
    rows = lax.broadcasted_iota(jnp.int32, (tk, tq), 0)
    cols = lax.broadcasted_iota(jnp.int32, (tk, tq), 1)

    def kv_step(k0, q0, masked):
        s = _dot(k_ref[pl.ds(k0, tk), :], qt_scr[...])
        if masked:
            s = jnp.where(k0 + rows <= q0 + cols, s, NEG_BIG)
        m_prev = m_scr[...]
        m_new = jnp.maximum(m_prev, jnp.max(s, axis=0, keepdims=True))
        alpha = jnp.exp2(m_prev - m_new)
        p = jnp.exp2(s - m_new)
        l_scr[...] = alpha * l_scr[...] + jnp.sum(p, axis=0, keepdims=True)
        acc_scr[...] = alpha * acc_scr[...] + _dot(vt_ref[:, pl.ds(k0, tk)], p.astype(BF16))
        m_scr[...] = m_new

    def q_block(i, carry):
        q0 = pl.multiple_of(i * tq, tq)
        qt_scr[...] = q_ref[pl.ds(q0, tq), :].astype(F32).T.astype(BF16)
        m_scr[...] = jnp.full_like(m_scr, NEG_BIG)
        l_scr[...] = jnp.zeros_like(l_scr)
        acc_scr[...] = jnp.zeros_like(acc_scr)
        n_full = (i * tq) // tk

        def full_step(j, c):
            kv_step(pl.multiple_of(j * tk, tk), q0, False)
            return c

        lax.fori_loop(0, n_full, full_step, 0)
        for d in range(tq // tk):
            kv_step(pl.multiple_of(q0 + d * tk, tk), q0, True)
        out = acc_scr[...] * (1.0 / l_scr[...])
        o_ref[pl.ds(q0, tq), :] = out.T.astype(o_ref.dtype)
        return carry

    lax.fori_loop(0, T // tq, q_block, 0)


def _attention(q, k, vt, *, tq, tk):
    B, T, _ = q.shape
    H = MLA_HEADS
    return pl.pallas_call(
        functools.partial(_attn_kernel, tq=tq, tk=tk),
        grid=(B, H),
        in_specs=[pl.BlockSpec((None, T, HEAD_SLOT), lambda b, h: (b, 0, h)),
                  pl.BlockSpec((None, T, HEAD_SLOT), lambda b, h: (b, 0, h)),
                  pl.BlockSpec((None, MLA_V, T), lambda b, h: (b, h, 0))],
        out_specs=pl.BlockSpec((None, T, MLA_V), lambda b, h: (b, 0, h)),
        out_shape=jax.ShapeDtypeStruct((B, T, H * MLA_V), BF16),
        scratch_shapes=[pltpu.VMEM((HEAD_SLOT, tq), BF16),
                        pltpu.VMEM((1, tq), F32),
                        pltpu.VMEM((1, tq), F32),
                        pltpu.VMEM((MLA_V, tq), F32)],
        compiler_params=_cparams(("arbitrary", "arbitrary")),
        name="mla_attention",
    )(q, k, vt)


def _layer_norm(y, g, b):
    mu = jnp.mean(y, axis=-1, keepdims=True)
    yc = y - mu
    var = jnp.mean(yc * yc, axis=-1, keepdims=True)
    return yc * lax.rsqrt(var + LN_EPS) * g + b


def _trunk_kernel(x_ref, ohg_ref, omla_ref, mod_ref, wout_ref, w1_ref, w2_ref,
                  ln1g_ref, ln1b_ref, ln2g_ref, ln2b_ref, out_ref, *, alpha, ff_chunk):
    x = x_ref[...]
    g_a = mod_ref[2:3, :]
    sh_m = mod_ref[3:4, :]
    sc_m = mod_ref[4:5, :]
    g_m = mod_ref[5:6, :]
    hw = ohg_ref.shape[-1]
    mix = _dot(ohg_ref[...], wout_ref[:hw, :]) + _dot(omla_ref[...], wout_ref[hw:, :])
    x1 = _layer_norm(alpha * x + (1.0 + g_a) * mix, ln1g_ref[...], ln1b_ref[...])
    u = (x1 * (1.0 + sc_m) + sh_m).astype(BF16)
    d_ff = w1_ref.shape[1]
    acc = jnp.zeros(x.shape, F32)
    for c in range(d_ff // ff_chunk):
        hcol = _dot(u, w1_ref[:, c * ff_chunk:(c + 1) * ff_chunk])
        hcol = jnp.square(jnp.maximum(hcol, 0.0)).astype(BF16)
        acc = acc + _dot(hcol, w2_ref[c * ff_chunk:(c + 1) * ff_chunk, :])
    out_ref[...] = _layer_norm(alpha * x1 + (1.0 + g_m) * acc, ln2g_ref[...], ln2b_ref[...])


def _trunk(x, ohg, omla, mod3, wout, w1, w2, ln1g, ln1b, ln2g, ln2b, *, tm, alpha):
    B, T, D = x.shape
    hw = ohg.shape[-1]
    full = lambda a: pl.BlockSpec(a.shape, lambda b, i: (0,) * a.ndim)
    return pl.pallas_call(
        functools.partial(_trunk_kernel, alpha=alpha, ff_chunk=1024),
        grid=(B, T // tm),
        in_specs=[pl.BlockSpec((None, tm, D), lambda b, i: (b, i, 0)),
                  pl.BlockSpec((None, tm, hw), lambda b, i: (b, i, 0)),
                  pl.BlockSpec((None, tm, hw), lambda b, i: (b, i, 0)),
                  pl.BlockSpec((None, N_MOD, D), lambda b, i: (b, 0, 0)),
                  full(wout), full(w1), full(w2), full(ln1g), full(ln1b), full(ln2g), full(ln2b)],
        out_specs=pl.BlockSpec((None, tm, D), lambda b, i: (b, i, 0)),
        out_shape=jax.ShapeDtypeStruct((B, T, D), F32),
        compiler_params=_cparams(("arbitrary", "arbitrary")),
        name="trunk",
    )(x, ohg, omla, mod3, wout, w1, w2, ln1g, ln1b, ln2g, ln2b)


def _prep_in_weights(w_in):
    base = 4 * HG_WIDTH + 2 * MLA_RANK
    half = MLA_ROPE // 2
    kpe = w_in[:, base:base + MLA_ROPE]
    kpe_sw = jnp.concatenate([kpe[:, half:], kpe[:, :half]], axis=1)
    return jnp.concatenate([w_in[:, :base], kpe, kpe_sw], axis=1).astype(BF16)


def _prep_q_weights(w_q_up):
    half = MLA_ROPE // 2
    per = MLA_NOPE + MLA_ROPE
    cols = []
    for h in range(MLA_HEADS):
        w = w_q_up[:, h * per:(h + 1) * per]
        pe = w[:, MLA_NOPE:]
        cols += [w[:, :MLA_NOPE], pe, pe[:, half:], pe[:, :half]]
    return jnp.concatenate(cols, axis=1).astype(BF16)


def _prep_kv_weights(w_kv_up):
    per = MLA_NOPE + MLA_V
    kn = [w_kv_up[:, h * per:h * per + MLA_NOPE] for h in range(MLA_HEADS)]
    vv = [w_kv_up[:, h * per + MLA_NOPE:(h + 1) * per] for h in range(MLA_HEADS)]
    return jnp.concatenate(kn + vv, axis=1).astype(BF16)


def kernel(x, c, positions, w_ada, b_ada, w_in, hg_lower_bounds, hg_norm_w, mla_q_norm_w, w_q_up,
           mla_kv_norm_w, w_kv_up, w_out, ln1_g, ln1_b, w_mlp_in, w_mlp_out, ln2_g, ln2_b):
    depth = w_ada.shape[0]
    assert depth == 1, "single-layer block"
    B, T, D = x.shape
    alpha = float((2.0 * depth) ** 0.25)
    row = lambda a: a.reshape(1, -1)

    tab = _rope_table(positions)
    mod3 = _ada_mod(c, w_ada[0], b_ada[0]).reshape(B, N_MOD, D)
    hq, hf, hi, hg, q, k, vt = _inproj(
        x, mod3, _prep_in_weights(w_in[0]), tab, row(mla_q_norm_w[0]), row(mla_kv_norm_w[0]),
        _prep_q_weights(w_q_up[0]), _prep_kv_weights(w_kv_up[0]), tm=512)
    ohg = _hgrn(hq, hf, hi, hg, hg_lower_bounds, row(hg_norm_w[0]), tt=256)
    omla = _attention(q, k, vt, tq=512, tk=512)
    return _trunk(x, ohg, omla, mod3, w_out[0].astype(BF16), w_mlp_in[0].astype(BF16),
                  w_mlp_out[0].astype(BF16), row(ln1_g[0]), row(ln1_b[0]), row(ln2_g[0]), row(ln2_b[0]),
                  tm=512, alpha=alpha)
```

```python
import functools
import math

import jax
import jax.numpy as jnp
from jax import lax
from jax.experimental import pallas as pl
from jax.experimental.pallas import tpu as pltpu

F32 = jnp.float32
BF16 = jnp.bfloat16

HG_HEADS = 4
HG_DIM = 128
HG_WIDTH = HG_HEADS * HG_DIM
HG_CHUNK = 64
MLA_HEADS = 4
MLA_NOPE = 128
MLA_ROPE = 64
MLA_V = 128
MLA_RANK = 256
ROPE_THETA = 10000.0
N_MOD = 6
RMS_EPS = 1e-6
LN_EPS = 1e-5

LANES = 128
V7X_VMEM_BYTES = 64 * 1024 * 1024
VMEM_LIMIT = V7X_VMEM_BYTES - 6 * 1024 * 1024

HEAD_SLOT = 2 * LANES
NEG_BIG = -1e30
LOG2E = 1.4426950408889634


def _sigmoid(x):
    return 1.0 / (1.0 + jnp.exp(-x))


def _split_bf16(x):
    hi = x.astype(BF16)
    lo = (x - hi.astype(F32)).astype(BF16)
    return hi, lo


def _dot(a, b):
    return jnp.dot(a, b, preferred_element_type=F32)


def _dot_nt(a, b):
    return lax.dot_general(a, b, (((1,), (1,)), ((), ())), preferred_element_type=F32)


def _dot_tn(a, b):
    return lax.dot_general(a, b, (((0,), (0,)), ((), ())), preferred_element_type=F32)


def _cparams(semantics):
    return pltpu.CompilerParams(dimension_semantics=semantics, vmem_limit_bytes=VMEM_LIMIT)


def _rope_table_kernel(pos_ref, invf_ref, out_ref):
    ang = invf_ref[...] * pos_ref[...].astype(F32)
    c = jnp.cos(ang)
    s = jnp.sin(ang)
    tab = jnp.concatenate([c, c, -s, s], axis=0)
    out_ref[...] = tab.T


def _rope_table(positions):
    B, T = positions.shape
    half = MLA_ROPE // 2
    inv_freq = 1.0 / (ROPE_THETA ** (jnp.arange(0, MLA_ROPE, 2, dtype=F32) / MLA_ROPE))
    return pl.pallas_call(
        _rope_table_kernel,
        grid=(B,),
        in_specs=[pl.BlockSpec((None, 1, T), lambda b: (b, 0, 0)),
                  pl.BlockSpec((half, 1), lambda b: (0, 0))],
        out_specs=pl.BlockSpec((None, T, LANES), lambda b: (b, 0, 0)),
        out_shape=jax.ShapeDtypeStruct((B, T, LANES), F32),
        compiler_params=_cparams(("arbitrary",)),
        name="rope_table",
    )(positions.reshape(B, 1, T), inv_freq.reshape(half, 1))


def _ada_kernel(c_ref, w_ref, b_ref, out_ref):
    c = c_ref[...]
    cond = c * _sigmoid(c)
    ch, cl = _split_bf16(cond)
    wh, wl = _split_bf16(w_ref[...])
    out_ref[...] = _dot(ch, wh) + (_dot(cl, wh) + _dot(ch, wl)) + b_ref[...]


def _ada_mod(c, w_ada, b_ada):
    B, D = c.shape
    n_out = w_ada.shape[1]
    tn = D
    return pl.pallas_call(
        _ada_kernel,
        grid=(n_out // tn,),
        in_specs=[pl.BlockSpec((B, D), lambda j: (0, 0)),
                  pl.BlockSpec((D, tn), lambda j: (0, j)),
                  pl.BlockSpec((1, tn), lambda j: (0, j))],
        out_specs=pl.BlockSpec((B, tn), lambda j: (0, j)),
        out_shape=jax.ShapeDtypeStruct((B, n_out), F32),
        compiler_params=_cparams(("arbitrary",)),
        name="ada_mod",
    )(c, w_ada, b_ada.reshape(1, n_out))


def _rms(v, w):
    return v * lax.rsqrt(jnp.mean(v * v, axis=-1, keepdims=True) + RMS_EPS) * w


def _inproj_kernel(x_ref, mod_ref, win_ref, tab_ref, qnw_ref, kvnw_ref, wq_ref, wkv_ref,
                   hq_ref, hf_ref, hi_ref, hg_ref, q_ref, k_ref, vt_ref, *, q_scale):
    x = x_ref[...]
    sh = mod_ref[0:1, :]
    sc = mod_ref[1:2, :]
    u = (x * (1.0 + sc) + sh).astype(BF16)
    for idx, ref in enumerate((hq_ref, hf_ref, hi_ref, hg_ref)):
        z = _dot(u, win_ref[:, idx * HG_WIDTH:(idx + 1) * HG_WIDTH])
        for h in range(HG_HEADS):
            ref[h] = z[:, h * HG_DIM:(h + 1) * HG_DIM].astype(BF16)
    lat0 = 4 * HG_WIDTH
    zl = _dot(u, win_ref[:, lat0:lat0 + 2 * MLA_RANK + LANES])
    cq = zl[:, :MLA_RANK]
    ckv = zl[:, MLA_RANK:2 * MLA_RANK]
    zk = zl[:, 2 * MLA_RANK:]
    tab = tab_ref[...]

    cqn = _rms(cq, qnw_ref[...]).astype(BF16)
    qf = _dot(cqn, wq_ref[...])
    for h in range(MLA_HEADS):
        base = h * HEAD_SLOT
        q_ref[:, base:base + LANES] = (qf[:, base:base + LANES] * q_scale).astype(BF16)
        t = qf[:, base + LANES:base + HEAD_SLOT] * tab
        r = (t + pltpu.roll(t, MLA_ROPE, 1)) * q_scale
        q_ref[:, base + LANES:base + HEAD_SLOT] = r.astype(BF16)

    ckvn = _rms(ckv, kvnw_ref[...]).astype(BF16)
    kv = _dot(ckvn, wkv_ref[...])
    tk = zk * tab
    lane = lax.broadcasted_iota(jnp.int32, tk.shape, 1)
    kpe = jnp.where(lane < MLA_ROPE, tk + pltpu.roll(tk, MLA_ROPE, 1), 0.0).astype(BF16)
    for h in range(MLA_HEADS):
        base = h * HEAD_SLOT
        k_ref[:, base:base + LANES] = kv[:, h * MLA_NOPE:(h + 1) * MLA_NOPE].astype(BF16)
        k_ref[:, base + LANES:base + HEAD_SLOT] = kpe
    vt_ref[...] = kv[:, MLA_HEADS * MLA_NOPE:].T.astype(BF16)


def _inproj(x, mod3, win, tab, qnw, kvnw, wq, wkv, *, tm):
    B, T, D = x.shape
    q_scale = float((MLA_NOPE + MLA_ROPE) ** -0.5 * LOG2E)
    hshape = jax.ShapeDtypeStruct((B, HG_HEADS, T, HG_DIM), BF16)
    hspec = pl.BlockSpec((None, HG_HEADS, tm, HG_DIM), lambda b, i: (b, 0, i, 0))
    qk_w = MLA_HEADS * HEAD_SLOT
    full = lambda a: pl.BlockSpec(a.shape, lambda b, i: (0,) * a.ndim)
    return pl.pallas_call(
        functools.partial(_inproj_kernel, q_scale=q_scale),
        grid=(B, T // tm),
        in_specs=[pl.BlockSpec((None, tm, D), lambda b, i: (b, i, 0)),
                  pl.BlockSpec((None, N_MOD, D), lambda b, i: (b, 0, 0)),
                  full(win),
                  pl.BlockSpec((None, tm, LANES), lambda b, i: (b, i, 0)),
                  full(qnw), full(kvnw), full(wq), full(wkv)],
        out_specs=[hspec, hspec, hspec, hspec,
                   pl.BlockSpec((None, tm, qk_w), lambda b, i: (b, i, 0)),
                   pl.BlockSpec((None, tm, qk_w), lambda b, i: (b, i, 0)),
                   pl.BlockSpec((None, MLA_HEADS * MLA_V, tm), lambda b, i: (b, 0, i))],
        out_shape=[hshape, hshape, hshape, hshape,
                   jax.ShapeDtypeStruct((B, T, qk_w), BF16),
                   jax.ShapeDtypeStruct((B, T, qk_w), BF16),
                   jax.ShapeDtypeStruct((B, MLA_HEADS * MLA_V, T), BF16)],
        compiler_params=_cparams(("arbitrary", "arbitrary")),
        name="inproj",
    )(x, mod3, win, tab, qnw, kvnw, wq, wkv)


def _hgrn_kernel(hq_ref, hf_ref, hi_ref, hg_ref, lbp_ref, nw_ref, tri_ref, o_ref, st_ref, *, tt):
    @pl.when(pl.program_id(1) == 0)
    def _():
        st_ref[...] = jnp.zeros_like(st_ref)

    lbp = lbp_ref[...]
    e = jnp.exp(lbp - jnp.max(lbp, axis=0, keepdims=True))
    lb_all = e[0:1, :] / jnp.sum(e, axis=0, keepdims=True)
    nw_all = nw_ref[...]
    tri = tri_ref[...]
    C = HG_CHUNK
    row = lax.broadcasted_iota(jnp.int32, (C, C), 0)
    col = lax.broadcasted_iota(jnp.int32, (C, C), 1)
    causal = col <= row

    for h in range(HG_HEADS):
        lb = lb_all[:, h * HG_DIM:(h + 1) * HG_DIM]
        nw = nw_all[:, h * HG_DIM:(h + 1) * HG_DIM]
        q = hq_ref[h].astype(F32)
        fl = hf_ref[h].astype(F32)
        v = hi_ref[h]
        g = hg_ref[h].astype(F32)
        f = lb + (1.0 - lb) * _sigmoid(fl)
        k = 1.0 - f
        lf_hi, lf_lo = _split_bf16(jnp.log(f))
        b = _dot(tri, lf_hi) + _dot(tri, lf_lo)
        st = st_ref[h]
        for c in range(tt // C):
            sl = slice(c * C, (c + 1) * C)
            bc = b[sl]
            bref = bc[C // 2 - 1:C // 2]
            blast = bc[C - 1:C]
            qc = q[sl]
            kc = k[sl]
            vc = v[sl]
            qd = (qc * jnp.exp(bc - bref)).astype(BF16)
            kd = (kc * jnp.exp(bref - bc)).astype(BF16)
            a = jnp.where(causal, _dot_nt(qd, kd), 0.0).astype(BF16)
            o = _dot(a, vc)
            qb = (qc * jnp.exp(bc)).astype(BF16)
            o = o + _dot_nt(qb, st.astype(BF16))
            kl = (kc * jnp.exp(blast - bc)).astype(BF16)
            st = st * jnp.exp(blast) + _dot_tn(vc, kl)
            o = o * lax.rsqrt(jnp.mean(o * o, axis=-1, keepdims=True) + RMS_EPS) * nw
            gc = g[sl]
            o = o * (gc * _sigmoid(gc))
            o_ref[sl, h * HG_DIM:(h + 1) * HG_DIM] = o.astype(o_ref.dtype)
        st_ref[h] = st


def _hgrn(hq, hf, hi, hg, lbp, nw, *, tt):
    B, H, T, d = hq.shape
    C = HG_CHUNK
    r = jnp.arange(tt)
    tri = ((r[:, None] // C == r[None, :] // C) & (r[None, :] <= r[:, None])).astype(BF16)
    hspec = pl.BlockSpec((None, H, tt, d), lambda b, i: (b, 0, i, 0))
    full = lambda a: pl.BlockSpec(a.shape, lambda b, i: (0,) * a.ndim)
    return pl.pallas_call(
        functools.partial(_hgrn_kernel, tt=tt),
        grid=(B, T // tt),
        in_specs=[hspec, hspec, hspec, hspec, full(lbp), full(nw), full(tri)],
        out_specs=pl.BlockSpec((None, tt, H * d), lambda b, i: (b, i, 0)),
        out_shape=jax.ShapeDtypeStruct((B, T, H * d), BF16),
        scratch_shapes=[pltpu.VMEM((H, d, d), F32)],
        compiler_params=_cparams(("arbitrary", "arbitrary")),
        name="hgrn2",
    )(hq, hf, hi, hg, lbp, nw, tri)


def _attn_kernel(q_ref, k_ref, vt_ref, o_ref, qt_scr, *, tq, tk):
    T = q_ref.shape[0]
    rows = lax.broadcasted_iota(jnp.int32, (tk, tq), 0)
    cols = lax.broadcasted_iota(jnp.int32, (tk, tq), 1)

    tasks = []
    for i in range(T // tq):
        nkv = ((i + 1) * tq) // tk
        for j in range(nkv):
            tasks.append((i, j, j == 0, j == nkv - 1))

    def scores(i, j):
        if j == 0:
            qt_scr[i % 2] = q_ref[i * tq:(i + 1) * tq, :].astype(F32).T.astype(BF16)
        s = _dot(k_ref[j * tk:(j + 1) * tk, :], qt_scr[i % 2])
        if (j + 1) * tk > i * tq:
            s = jnp.where(j * tk + rows <= i * tq + cols, s, NEG_BIG)
        return s

    s_next = scores(*tasks[0][:2])
    m = l = acc = None
    for n, (i, j, first, last) in enumerate(tasks):
        s = s_next
        if n + 1 < len(tasks):
            s_next = scores(*tasks[n + 1][:2])
        smax = jnp.max(s, axis=0, keepdims=True)
        if first:
            m_new = smax
            p = jnp.exp2(s - m_new)
            l = jnp.sum(p, axis=0, keepdims=True)
            acc = _dot(vt_ref[:, j * tk:(j + 1) * tk], p.astype(BF16))
        else:
            m_new = jnp.maximum(m, smax)
            alpha = jnp.exp2(m - m_new)
            p = jnp.exp2(s - m_new)
            l = alpha * l + jnp.sum(p, axis=0, keepdims=True)
            acc = alpha * acc + _dot(vt_ref[:, j * tk:(j + 1) * tk], p.astype(BF16))
        m = m_new
        if last:
            out = acc * (1.0 / l)
            o_ref[i * tq:(i + 1) * tq, :] = out.T.astype(o_ref.dtype)


def _attention(q, k, vt, *, tq, tk):
    B, T, _ = q.shape
    H = MLA_HEADS
    return pl.pallas_call(
        functools.partial(_attn_kernel, tq=tq, tk=tk),
        grid=(B, H),
        in_specs=[pl.BlockSpec((None, T, HEAD_SLOT), lambda b, h: (b, 0, h)),
                  pl.BlockSpec((None, T, HEAD_SLOT), lambda b, h: (b, 0, h)),
                  pl.BlockSpec((None, MLA_V, T), lambda b, h: (b, h, 0))],
        out_specs=pl.BlockSpec((None, T, MLA_V), lambda b, h: (b, 0, h)),
        out_shape=jax.ShapeDtypeStruct((B, T, H * MLA_V), BF16),
        scratch_shapes=[pltpu.VMEM((2, HEAD_SLOT, tq), BF16)],
        compiler_params=_cparams(("arbitrary", "arbitrary")),
        name="mla_attention",
    )(q, k, vt)


def _layer_norm(y, g, b):
    mu = jnp.mean(y, axis=-1, keepdims=True)
    yc = y - mu
    var = jnp.mean(yc * yc, axis=-1, keepdims=True)
    return yc * lax.rsqrt(var + LN_EPS) * g + b


def _trunk_kernel(x_ref, ohg_ref, omla_ref, mod_ref, wout_ref, w1_ref, w2_ref,
                  ln1g_ref, ln1b_ref, ln2g_ref, ln2b_ref, out_ref, *, alpha, ff_chunk):
    x = x_ref[...]
    g_a = mod_ref[2:3, :]
    sh_m = mod_ref[3:4, :]
    sc_m = mod_ref[4:5, :]
    g_m = mod_ref[5:6, :]
    hw = ohg_ref.shape[-1]
    mix = _dot(ohg_ref[...], wout_ref[:hw, :]) + _dot(omla_ref[...], wout_ref[hw:, :])
    x1 = _layer_norm(alpha * x + (1.0 + g_a) * mix, ln1g_ref[...], ln1b_ref[...])
    u = (x1 * (1.0 + sc_m) + sh_m).astype(BF16)
    d_ff = w1_ref.shape[1]
    acc = jnp.zeros(x.shape, F32)
    for c in range(d_ff // ff_chunk):
        hcol = _dot(u, w1_ref[:, c * ff_chunk:(c + 1) * ff_chunk])
        hcol = jnp.square(jnp.maximum(hcol, 0.0)).astype(BF16)
        acc = acc + _dot(hcol, w2_ref[c * ff_chunk:(c + 1) * ff_chunk, :])
    out_ref[...] = _layer_norm(alpha * x1 + (1.0 + g_m) * acc, ln2g_ref[...], ln2b_ref[...])


def _trunk(x, ohg, omla, mod3, wout, w1, w2, ln1g, ln1b, ln2g, ln2b, *, tm, alpha):
    B, T, D = x.shape
    hw = ohg.shape[-1]
    full = lambda a: pl.BlockSpec(a.shape, lambda b, i: (0,) * a.ndim)
    return pl.pallas_call(
        functools.partial(_trunk_kernel, alpha=alpha, ff_chunk=1024),
        grid=(B, T // tm),
        in_specs=[pl.BlockSpec((None, tm, D), lambda b, i: (b, i, 0)),
                  pl.BlockSpec((None, tm, hw), lambda b, i: (b, i, 0)),
                  pl.BlockSpec((None, tm, hw), lambda b, i: (b, i, 0)),
                  pl.BlockSpec((None, N_MOD, D), lambda b, i: (b, 0, 0)),
                  full(wout), full(w1), full(w2), full(ln1g), full(ln1b), full(ln2g), full(ln2b)],
        out_specs=pl.BlockSpec((None, tm, D), lambda b, i: (b, i, 0)),
        out_shape=jax.ShapeDtypeStruct((B, T, D), F32),
        compiler_params=_cparams(("arbitrary", "arbitrary")),
        name="trunk",
    )(x, ohg, omla, mod3, wout, w1, w2, ln1g, ln1b, ln2g, ln2b)


def _prep_in_weights(w_in):
    base = 4 * HG_WIDTH + 2 * MLA_RANK
    half = MLA_ROPE // 2
    kpe = w_in[:, base:base + MLA_ROPE]
    kpe_sw = jnp.concatenate([kpe[:, half:], kpe[:, :half]], axis=1)
    return jnp.concatenate([w_in[:, :base], kpe, kpe_sw], axis=1).astype(BF16)


def _prep_q_weights(w_q_up):
    half = MLA_ROPE // 2
    per = MLA_NOPE + MLA_ROPE
    cols = []
    for h in range(MLA_HEADS):
        w = w_q_up[:, h * per:(h + 1) * per]
        pe = w[:, MLA_NOPE:]
        cols += [w[:, :MLA_NOPE], pe, pe[:, half:], pe[:, :half]]
    return jnp.concatenate(cols, axis=1).astype(BF16)


def _prep_kv_weights(w_kv_up):
    per = MLA_NOPE + MLA_V
    kn = [w_kv_up[:, h * per:h * per + MLA_NOPE] for h in range(MLA_HEADS)]
    vv = [w_kv_up[:, h * per + MLA_NOPE:(h + 1) * per] for h in range(MLA_HEADS)]
    return jnp.concatenate(kn + vv, axis=1).astype(BF16)


def kernel(x, c, positions, w_ada, b_ada, w_in, hg_lower_bounds, hg_norm_w, mla_q_norm_w, w_q_up,
           mla_kv_norm_w, w_kv_up, w_out, ln1_g, ln1_b, w_mlp_in, w_mlp_out, ln2_g, ln2_b):
    depth = w_ada.shape[0]
    assert depth == 1, "single-layer block"
    B, T, D = x.shape
    alpha = float((2.0 * depth) ** 0.25)
    row = lambda a: a.reshape(1, -1)

    tab = _rope_table(positions)
    mod3 = _ada_mod(c, w_ada[0], b_ada[0]).reshape(B, N_MOD, D)
    hq, hf, hi, hg, q, k, vt = _inproj(
        x, mod3, _prep_in_weights(w_in[0]), tab, row(mla_q_norm_w[0]), row(mla_kv_norm_w[0]),
        _prep_q_weights(w_q_up[0]), _prep_kv_weights(w_kv_up[0]), tm=512)
    ohg = _hgrn(hq, hf, hi, hg, hg_lower_bounds, row(hg_norm_w[0]), tt=256)
    omla = _attention(q, k, vt, tq=512, tk=512)
    return _trunk(x, ohg, omla, mod3, w_out[0].astype(BF16), w_mlp_in[0].astype(BF16),
                  w_mlp_out[0].astype(BF16), row(ln1_g[0]), row(ln1_b[0]), row(ln2_g[0]), row(ln2_b[0]),
                  tm=512, alpha=alpha)
```

```python
import functools
import math

import jax
import jax.numpy as jnp
from jax import lax
from jax.experimental import pallas as pl
from jax.experimental.pallas import tpu as pltpu

F32 = jnp.float32
BF16 = jnp.bfloat16

HG_HEADS = 4
HG_DIM = 128
HG_WIDTH = HG_HEADS * HG_DIM
HG_CHUNK = 64
MLA_HEADS = 4
MLA_NOPE = 128
MLA_ROPE = 64
MLA_V = 128
MLA_RANK = 256
ROPE_THETA = 10000.0
N_MOD = 6
RMS_EPS = 1e-6
LN_EPS = 1e-5

LANES = 128
V7X_VMEM_BYTES = 64 * 1024 * 1024
VMEM_LIMIT = V7X_VMEM_BYTES - 6 * 1024 * 1024

HEAD_SLOT = 2 * LANES
NEG_BIG = -1e30
LOG2E = 1.4426950408889634


def _sigmoid(x):
    return 1.0 / (1.0 + jnp.exp(-x))


def _split_bf16(x):
    hi = x.astype(BF16)
    lo = (x - hi.astype(F32)).astype(BF16)
    return hi, lo


def _dot(a, b):
    return jnp.dot(a, b, preferred_element_type=F32)


def _dot_nt(a, b):
    return lax.dot_general(a, b, (((1,), (1,)), ((), ())), preferred_element_type=F32)


def _dot_tn(a, b):
    return lax.dot_general(a, b, (((0,), (0,)), ((), ())), preferred_element_type=F32)


def _cparams(semantics):
    return pltpu.CompilerParams(dimension_semantics=semantics, vmem_limit_bytes=VMEM_LIMIT)


def _rope_table_kernel(pos_ref, invf_ref, out_ref):
    ang = invf_ref[...] * pos_ref[...].astype(F32)
    c = jnp.cos(ang)
    s = jnp.sin(ang)
    tab = jnp.concatenate([c, c, -s, s], axis=0)
    out_ref[...] = tab.T


def _rope_table(positions):
    B, T = positions.shape
    half = MLA_ROPE // 2
    inv_freq = 1.0 / (ROPE_THETA ** (jnp.arange(0, MLA_ROPE, 2, dtype=F32) / MLA_ROPE))
    return pl.pallas_call(
        _rope_table_kernel,
        grid=(B,),
        in_specs=[pl.BlockSpec((None, 1, T), lambda b: (b, 0, 0)),
                  pl.BlockSpec((half, 1), lambda b: (0, 0))],
        out_specs=pl.BlockSpec((None, T, LANES), lambda b: (b, 0, 0)),
        out_shape=jax.ShapeDtypeStruct((B, T, LANES), F32),
        compiler_params=_cparams(("arbitrary",)),
        name="rope_table",
    )(positions.reshape(B, 1, T), inv_freq.reshape(half, 1))


def _ada_kernel(c_ref, w_ref, b_ref, out_ref):
    c = c_ref[...]
    cond = c * _sigmoid(c)
    ch, cl = _split_bf16(cond)
    wh, wl = _split_bf16(w_ref[...])
    out_ref[...] = _dot(ch, wh) + (_dot(cl, wh) + _dot(ch, wl)) + b_ref[...]


def _ada_mod(c, w_ada, b_ada):
    B, D = c.shape
    n_out = w_ada.shape[1]
    tn = D
    return pl.pallas_call(
        _ada_kernel,
        grid=(n_out // tn,),
        in_specs=[pl.BlockSpec((B, D), lambda j: (0, 0)),
                  pl.BlockSpec((D, tn), lambda j: (0, j)),
                  pl.BlockSpec((1, tn), lambda j: (0, j))],
        out_specs=pl.BlockSpec((B, tn), lambda j: (0, j)),
        out_shape=jax.ShapeDtypeStruct((B, n_out), F32),
        compiler_params=_cparams(("arbitrary",)),
        name="ada_mod",
    )(c, w_ada, b_ada.reshape(1, n_out))


def _rms(v, w):
    return v * lax.rsqrt(jnp.mean(v * v, axis=-1, keepdims=True) + RMS_EPS) * w


def _inproj_kernel(x_ref, mod_ref, win_ref, tab_ref, qnw_ref, kvnw_ref, wq_ref, wkv_ref,
                   hq_ref, hf_ref, hi_ref, hg_ref, q_ref, k_ref, vt_ref, *, q_scale):
    x = x_ref[...]
    sh = mod_ref[0:1, :]
    sc = mod_ref[1:2, :]
    u = (x * (1.0 + sc) + sh).astype(BF16)
    for idx, ref in enumerate((hq_ref, hf_ref, hi_ref, hg_ref)):
        ref[...] = _dot(u, win_ref[:, idx * HG_WIDTH:(idx + 1) * HG_WIDTH]).astype(BF16)
    lat0 = 4 * HG_WIDTH
    zl = _dot(u, win_ref[:, lat0:lat0 + 2 * MLA_RANK + LANES])
    cq = zl[:, :MLA_RANK]
    ckv = zl[:, MLA_RANK:2 * MLA_RANK]
    zk = zl[:, 2 * MLA_RANK:]
    tab = tab_ref[...]

    cqn = _rms(cq, qnw_ref[...]).astype(BF16)
    qf = _dot(cqn, wq_ref[...])
    for h in range(MLA_HEADS):
        base = h * HEAD_SLOT
        q_ref[:, base:base + LANES] = (qf[:, base:base + LANES] * q_scale).astype(BF16)
        t = qf[:, base + LANES:base + HEAD_SLOT] * tab
        r = (t + pltpu.roll(t, MLA_ROPE, 1)) * q_scale
        q_ref[:, base + LANES:base + HEAD_SLOT] = r.astype(BF16)

    ckvn = _rms(ckv, kvnw_ref[...]).astype(BF16)
    kv = _dot(ckvn, wkv_ref[...])
    tk = zk * tab
    lane = lax.broadcasted_iota(jnp.int32, tk.shape, 1)
    kpe = jnp.where(lane < MLA_ROPE, tk + pltpu.roll(tk, MLA_ROPE, 1), 0.0).astype(BF16)
    for h in range(MLA_HEADS):
        base = h * HEAD_SLOT
        k_ref[:, base:base + LANES] = kv[:, h * MLA_NOPE:(h + 1) * MLA_NOPE].astype(BF16)
        k_ref[:, base + LANES:base + HEAD_SLOT] = kpe
    vt_ref[...] = kv[:, MLA_HEADS * MLA_NOPE:].T.astype(BF16)


def _inproj(x, mod3, win, tab, qnw, kvnw, wq, wkv, *, tm):
    B, T, D = x.shape
    q_scale = float((MLA_NOPE + MLA_ROPE) ** -0.5 * LOG2E)
    hshape = jax.ShapeDtypeStruct((B, T, HG_WIDTH), BF16)
    hspec = pl.BlockSpec((None, tm, HG_WIDTH), lambda b, i: (b, i, 0))
    qk_w = MLA_HEADS * HEAD_SLOT
    full = lambda a: pl.BlockSpec(a.shape, lambda b, i: (0,) * a.ndim)
    return pl.pallas_call(
        functools.partial(_inproj_kernel, q_scale=q_scale),
        grid=(B, T // tm),
        in_specs=[pl.BlockSpec((None, tm, D), lambda b, i: (b, i, 0)),
                  pl.BlockSpec((None, N_MOD, D), lambda b, i: (b, 0, 0)),
                  full(win),
                  pl.BlockSpec((None, tm, LANES), lambda b, i: (b, i, 0)),
                  full(qnw), full(kvnw), full(wq), full(wkv)],
        out_specs=[hspec, hspec, hspec, hspec,
                   pl.BlockSpec((None, tm, qk_w), lambda b, i: (b, i, 0)),
                   pl.BlockSpec((None, tm, qk_w), lambda b, i: (b, i, 0)),
                   pl.BlockSpec((None, MLA_HEADS * MLA_V, tm), lambda b, i: (b, 0, i))],
        out_shape=[hshape, hshape, hshape, hshape,
                   jax.ShapeDtypeStruct((B, T, qk_w), BF16),
                   jax.ShapeDtypeStruct((B, T, qk_w), BF16),
                   jax.ShapeDtypeStruct((B, MLA_HEADS * MLA_V, T), BF16)],
        compiler_params=_cparams(("arbitrary", "arbitrary")),
        name="inproj",
    )(x, mod3, win, tab, qnw, kvnw, wq, wkv)


HG_GROUP = 4 * HG_CHUNK


def _hgrn_kernel(q_ref, f_ref, v_ref, g_ref, lbp_ref, nw_ref, tri_ref, trif_ref, o_ref, st_ref, *, tt):
    @pl.when(pl.program_id(1) == 0)
    def _():
        st_ref[...] = jnp.zeros_like(st_ref)

    C, G, W = HG_CHUNK, HG_GROUP, HG_WIDTH
    lbp = lbp_ref[...]
    e = jnp.exp(lbp - jnp.max(lbp, axis=0, keepdims=True))
    lb = e[0:1, :] / jnp.sum(e, axis=0, keepdims=True)
    f_mid = 0.5 * (1.0 + lb)
    f_amp = 0.5 * (1.0 - lb)
    nw = nw_ref[...]
    tri = tri_ref[...]
    in_chunk_causal = trif_ref[...] > 0.5

    def decay_stage(g0):
        rows = slice(g0, g0 + G)
        f = f_mid + f_amp * jnp.tanh(0.5 * f_ref[rows, :].astype(F32))
        kk = 1.0 - f
        l_hi, l_lo = _split_bf16(jnp.log2(f))
        b2 = _dot(tri, jnp.concatenate([l_hi, l_lo], axis=1))
        b2 = b2[:, :W] + b2[:, W:]
        q = q_ref[rows, :].astype(F32)
        qd, kd, qb, kl, dec = [], [], [], [], []
        for c in range(G // C):
            sl = slice(c * C, (c + 1) * C)
            b2c = b2[sl]
            ref = b2c[C // 2 - 1:C // 2]
            last = b2c[C - 1:C]
            d = b2c - ref
            qdf = q[sl] * jnp.exp2(d)
            kdf = kk[sl] * jnp.exp2(-d)
            qd.append(qdf.astype(BF16))
            kd.append(kdf.astype(BF16))
            qb.append((qdf * jnp.exp2(ref)).astype(BF16))
            kl.append((kdf * jnp.exp2(last - ref)).astype(BF16))
            dec.append(jnp.exp2(last))
        return jnp.concatenate(qd, axis=0), jnp.concatenate(kd, axis=0), qb, kl, dec

    def matmul_stage(g0, qd, kd, qb, kl, dec):
        rows = slice(g0, g0 + G)
        for h in range(HG_HEADS):
            cols = slice(h * HG_DIM, (h + 1) * HG_DIM)
            v = v_ref[rows, cols]
            a = jnp.where(in_chunk_causal, _dot_nt(qd[:, cols], kd[:, cols]), 0.0).astype(BF16)
            o_intra = _dot(a, v)
            kvs = [_dot_tn(v[c * C:(c + 1) * C], kl[c][:, cols]) for c in range(G // C)]
            o_inter = []
            for c in range(G // C):
                o_inter.append(_dot_nt(qb[c][:, cols], st[h].astype(BF16)))
                st[h] = st[h] * dec[c][:, cols] + kvs[c]
            o = o_intra + jnp.concatenate(o_inter, axis=0)
            o = o * lax.rsqrt(jnp.mean(o * o, axis=-1, keepdims=True) + RMS_EPS) * nw[:, cols]
            hg = 0.5 * g_ref[rows, cols].astype(F32)
            o = o * (hg * (1.0 + jnp.tanh(hg)))
            o_ref[rows, cols] = o.astype(o_ref.dtype)

    st = [st_ref[h] for h in range(HG_HEADS)]
    starts = list(range(0, tt, G))
    staged = decay_stage(starts[0])
    for n, g0 in enumerate(starts):
        cur = staged
        if n + 1 < len(starts):
            staged = decay_stage(starts[n + 1])
        matmul_stage(g0, *cur)
    for h in range(HG_HEADS):
        st_ref[h] = st[h]


def _hgrn(hq, hf, hi, hg, lbp, nw, *, tt):
    B, T, W = hq.shape
    C, G = HG_CHUNK, HG_GROUP
    r = jnp.arange(G)
    pattern = (r[:, None] // C == r[None, :] // C) & (r[None, :] <= r[:, None])
    tri = pattern.astype(BF16)
    trif = pattern.astype(F32)
    hspec = pl.BlockSpec((None, tt, W), lambda b, i: (b, i, 0))
    full = lambda a: pl.BlockSpec(a.shape, lambda b, i: (0,) * a.ndim)
    return pl.pallas_call(
        functools.partial(_hgrn_kernel, tt=tt),
        grid=(B, T // tt),
        in_specs=[hspec, hspec, hspec, hspec, full(lbp), full(nw), full(tri), full(trif)],
        out_specs=pl.BlockSpec((None, tt, W), lambda b, i: (b, i, 0)),
        out_shape=jax.ShapeDtypeStruct((B, T, W), BF16),
        scratch_shapes=[pltpu.VMEM((HG_HEADS, HG_DIM, HG_DIM), F32)],
        compiler_params=_cparams(("arbitrary", "arbitrary")),
        name="hgrn2",
    )(hq, hf, hi, hg, lbp, nw, tri, trif)


def _attn_kernel(q_ref, k_ref, vt_ref, o_ref, qt_scr, *, tq, tk):
    T = q_ref.shape[0]
    rows = lax.broadcasted_iota(jnp.int32, (tk, tq), 0)
    cols = lax.broadcasted_iota(jnp.int32, (tk, tq), 1)

    tasks = []
    for i in range(T // tq):
        nkv = ((i + 1) * tq) // tk
        for j in range(nkv):
            tasks.append((i, j, j == 0, j == nkv - 1))

    def scores(i, j):
        if j == 0:
            qt_scr[i % 2] = q_ref[i * tq:(i + 1) * tq, :].astype(F32).T.astype(BF16)
        s = _dot(k_ref[j * tk:(j + 1) * tk, :], qt_scr[i % 2])
        if (j + 1) * tk > i * tq:
            s = jnp.where(j * tk + rows <= i * tq + cols, s, NEG_BIG)
        return s

    s_next = scores(*tasks[0][:2])
    m = l = acc = None
    for n, (i, j, first, last) in enumerate(tasks):
        s = s_next
        if n + 1 < len(tasks):
            s_next = scores(*tasks[n + 1][:2])
        smax = jnp.max(s, axis=0, keepdims=True)
        if first:
            m_new = smax
            p = jnp.exp2(s - m_new)
            l = jnp.sum(p, axis=0, keepdims=True)
            acc = _dot(vt_ref[:, j * tk:(j + 1) * tk], p.astype(BF16))
        else:
            m_new = jnp.maximum(m, smax)
            alpha = jnp.exp2(m - m_new)
            p = jnp.exp2(s - m_new)
            l = alpha * l + jnp.sum(p, axis=0, keepdims=True)
            acc = alpha * acc + _dot(vt_ref[:, j * tk:(j + 1) * tk], p.astype(BF16))
        m = m_new
        if last:
            out = acc * (1.0 / l)
            o_ref[i * tq:(i + 1) * tq, :] = out.T.astype(o_ref.dtype)


def _attention(q, k, vt, *, tq, tk):
    B, T, _ = q.shape
    H = MLA_HEADS
    return pl.pallas_call(
        functools.partial(_attn_kernel, tq=tq, tk=tk),
        grid=(B, H),
        in_specs=[pl.BlockSpec((None, T, HEAD_SLOT), lambda b, h: (b, 0, h)),
                  pl.BlockSpec((None, T, HEAD_SLOT), lambda b, h: (b, 0, h)),
                  pl.BlockSpec((None, MLA_V, T), lambda b, h: (b, h, 0))],
        out_specs=pl.BlockSpec((None, T, MLA_V), lambda b, h: (b, 0, h)),
        out_shape=jax.ShapeDtypeStruct((B, T, H * MLA_V), BF16),
        scratch_shapes=[pltpu.VMEM((2, HEAD_SLOT, tq), BF16)],
        compiler_params=_cparams(("arbitrary", "arbitrary")),
        name="mla_attention",
    )(q, k, vt)


def _layer_norm(y, g, b):
    mu = jnp.mean(y, axis=-1, keepdims=True)
    yc = y - mu
    var = jnp.mean(yc * yc, axis=-1, keepdims=True)
    return yc * lax.rsqrt(var + LN_EPS) * g + b


def _trunk_kernel(x_ref, ohg_ref, omla_ref, mod_ref, wout_ref, w1_ref, w2_ref,
                  ln1g_ref, ln1b_ref, ln2g_ref, ln2b_ref, out_ref, *, alpha, ff_chunk):
    x = x_ref[...]
    g_a = mod_ref[2:3, :]
    sh_m = mod_ref[3:4, :]
    sc_m = mod_ref[4:5, :]
    g_m = mod_ref[5:6, :]
    hw = ohg_ref.shape[-1]
    mix = _dot(ohg_ref[...], wout_ref[:hw, :]) + _dot(omla_ref[...], wout_ref[hw:, :])
    x1 = _layer_norm(alpha * x + (1.0 + g_a) * mix, ln1g_ref[...], ln1b_ref[...])
    u = (x1 * (1.0 + sc_m) + sh_m).astype(BF16)
    d_ff = w1_ref.shape[1]
    acc = jnp.zeros(x.shape, F32)
    for c in range(d_ff // ff_chunk):
        hcol = _dot(u, w1_ref[:, c * ff_chunk:(c + 1) * ff_chunk])
        hcol = jnp.square(jnp.maximum(hcol, 0.0)).astype(BF16)
        acc = acc + _dot(hcol, w2_ref[c * ff_chunk:(c + 1) * ff_chunk, :])
    out_ref[...] = _layer_norm(alpha * x1 + (1.0 + g_m) * acc, ln2g_ref[...], ln2b_ref[...])


def _trunk(x, ohg, omla, mod3, wout, w1, w2, ln1g, ln1b, ln2g, ln2b, *, tm, alpha):
    B, T, D = x.shape
    hw = ohg.shape[-1]
    full = lambda a: pl.BlockSpec(a.shape, lambda b, i: (0,) * a.ndim)
    return pl.pallas_call(
        functools.partial(_trunk_kernel, alpha=alpha, ff_chunk=1024),
        grid=(B, T // tm),
        in_specs=[pl.BlockSpec((None, tm, D), lambda b, i: (b, i, 0)),
                  pl.BlockSpec((None, tm, hw), lambda b, i: (b, i, 0)),
                  pl.BlockSpec((None, tm, hw), lambda b, i: (b, i, 0)),
                  pl.BlockSpec((None, N_MOD, D), lambda b, i: (b, 0, 0)),
                  full(wout), full(w1), full(w2), full(ln1g), full(ln1b), full(ln2g), full(ln2b)],
        out_specs=pl.BlockSpec((None, tm, D), lambda b, i: (b, i, 0)),
        out_shape=jax.ShapeDtypeStruct((B, T, D), F32),
        compiler_params=_cparams(("arbitrary", "arbitrary")),
        name="trunk",
    )(x, ohg, omla, mod3, wout, w1, w2, ln1g, ln1b, ln2g, ln2b)


def _prep_in_weights(w_in):
    base = 4 * HG_WIDTH + 2 * MLA_RANK
    half = MLA_ROPE // 2
    kpe = w_in[:, base:base + MLA_ROPE]
    kpe_sw = jnp.concatenate([kpe[:, half:], kpe[:, :half]], axis=1)
    return jnp.concatenate([w_in[:, :base], kpe, kpe_sw], axis=1).astype(BF16)


def _prep_q_weights(w_q_up):
    half = MLA_ROPE // 2
    per = MLA_NOPE + MLA_ROPE
    cols = []
    for h in range(MLA_HEADS):
        w = w_q_up[:, h * per:(h + 1) * per]
        pe = w[:, MLA_NOPE:]
        cols += [w[:, :MLA_NOPE], pe, pe[:, half:], pe[:, :half]]
    return jnp.concatenate(cols, axis=1).astype(BF16)


def _prep_kv_weights(w_kv_up):
    per = MLA_NOPE + MLA_V
    kn = [w_kv_up[:, h * per:h * per + MLA_NOPE] for h in range(MLA_HEADS)]
    vv = [w_kv_up[:, h * per + MLA_NOPE:(h + 1) * per] for h in range(MLA_HEADS)]
    return jnp.concatenate(kn + vv, axis=1).astype(BF16)


def kernel(x, c, positions, w_ada, b_ada, w_in, hg_lower_bounds, hg_norm_w, mla_q_norm_w, w_q_up,
           mla_kv_norm_w, w_kv_up, w_out, ln1_g, ln1_b, w_mlp_in, w_mlp_out, ln2_g, ln2_b):
    depth = w_ada.shape[0]
    assert depth == 1, "single-layer block"
    B, T, D = x.shape
    alpha = float((2.0 * depth) ** 0.25)
    row = lambda a: a.reshape(1, -1)

    tab = _rope_table(positions)
    mod3 = _ada_mod(c, w_ada[0], b_ada[0]).reshape(B, N_MOD, D)
    hq, hf, hi, hg, q, k, vt = _inproj(
        x, mod3, _prep_in_weights(w_in[0]), tab, row(mla_q_norm_w[0]), row(mla_kv_norm_w[0]),
        _prep_q_weights(w_q_up[0]), _prep_kv_weights(w_kv_up[0]), tm=512)
    ohg = _hgrn(hq, hf, hi, hg, hg_lower_bounds, row(hg_norm_w[0]), tt=1024)
    omla = _attention(q, k, vt, tq=512, tk=512)
    return _trunk(x, ohg, omla, mod3, w_out[0].astype(BF16), w_mlp_in[0].astype(BF16),
                  w_mlp_out[0].astype(BF16), row(ln1_g[0]), row(ln1_b[0]), row(ln2_g[0]), row(ln2_b[0]),
                  tm=512, alpha=alpha)
```
